```python
import jax, jax.numpy as jnp
from jax import lax
import numpy as np

D_MODEL = 1024
BATCH = 8
SEQ = 8192
DEPTH = 1
DEC_BATCH = 2
DEC_SEQ = 8192
PAST_LEN = 128

D_MIX = D_MODEL
D_POOL = D_MIX // 2
D_CONV = D_MIX - D_POOL
POOL_WINDOWS = (2, 4, 8, 16)
N_POOL_GROUPS = len(POOL_WINDOWS)
POOL_GROUP = D_POOL // N_POOL_GROUPS
CONV_HEADS = 8
CONV_WIDTH = 3
D_PLE = 256
D_IN = 2 * D_POOL + 4 * D_CONV
EPS = 1e-6

kernel_name = "hybrid_pool_shortconv_encoder"


def _rmsnorm(x, g):
    xf = x.astype(jnp.float32)
    y = xf * lax.rsqrt(jnp.mean(xf * xf, axis=-1, keepdims=True) + EPS) * g.astype(jnp.float32)
    return y.astype(x.dtype)


def _pool_mixer(v, w_pool, scale):
    B, L, C = v.shape
    vf = v.astype(jnp.float32)
    cs = jnp.concatenate([jnp.zeros((B, 1, C), jnp.float32), jnp.cumsum(vf, axis=1)], axis=1)
    t = jnp.arange(L)
    outs = []
    for g, w in enumerate(POOL_WINDOWS):
        left, right = (w - 1) // 2, w // 2
        lo = jnp.clip(t - left, 0, L)
        hi = jnp.clip(t + right + 1, 0, L)
        c0 = g * POOL_GROUP
        csg = cs[:, :, c0:c0 + POOL_GROUP]
        mean = (csg[:, hi] - csg[:, lo]) / (hi - lo).astype(jnp.float32)[None, :, None]
        outs.append(mean - vf[:, :, c0:c0 + POOL_GROUP])
    d = jnp.stack(outs, axis=2).astype(v.dtype)
    y = jnp.einsum('blgc,gcd->blgd', d, w_pool).reshape(B, L, C)
    return y * scale


def _short_conv(u, gb, gc, conv_w):
    v = gc * u
    vp = jnp.pad(v, ((0, 0), (1, 1), (0, 0)))
    y = vp[:, :-2] * conv_w[0] + vp[:, 1:-1] * conv_w[1] + vp[:, 2:] * conv_w[2]
    return gb * y


def _layer(x, p, g_pre, w_in, w_pool, pool_scale, conv_w, w_out, g_post, w_ple, w_ple_gate, g_ple):
    h = _rmsnorm(x, g_pre)
    z = jnp.einsum('bsd,de->bse', h, w_in)
    a_val, a_gate, b_u, b_B, b_C, b_gate = jnp.split(z, 6, axis=-1)
    ya = _pool_mixer(a_val, w_pool, pool_scale) * jax.nn.silu(a_gate)
    yb = _short_conv(b_u, b_B, b_C, conv_w) * jax.nn.silu(b_gate)
    o = jnp.einsum('bse,ed->bsd', jnp.concatenate([ya, yb], axis=-1), w_out)
    x = x + _rmsnorm(o, g_post)
    gate = jax.nn.sigmoid(jnp.einsum('bsd,de->bse', x, w_ple_gate))
    e = jnp.einsum('bsk,kd->bsd', p, w_ple) * gate
    return x + _rmsnorm(e, g_ple)


def _trunk(x, p, g_pre, w_in, w_pool, pool_scale, conv_w, w_out, g_post, w_ple, w_ple_gate, g_ple):
    for i in range(DEPTH):
        x = _layer(x, p[i], g_pre[i], w_in[i], w_pool[i], pool_scale[i], conv_w[i], w_out[i],
                   g_post[i], w_ple[i], w_ple_gate[i], g_ple[i])
    return x


def setup_inputs(seed: int = 0) -> dict:
    key = jax.random.key(seed)
    ks = jax.random.split(key, 16)
    f32 = jnp.float32
    nrm = lambda k, s, sc: jax.random.normal(k, s, f32) * sc
    return {
        "x_prompt": nrm(ks[0], (BATCH, SEQ, D_MODEL), 1.0),
        "x_sample": nrm(ks[1], (DEC_BATCH, DEC_SEQ, D_MODEL), 1.0),
        "p_prompt": nrm(ks[2], (DEPTH, BATCH, SEQ, D_PLE), 1.0),
        "p_sample": nrm(ks[3], (DEPTH, DEC_BATCH, DEC_SEQ, D_PLE), 1.0),
        "g_pre": 1.0 + nrm(ks[4], (DEPTH, D_MODEL), 0.02),
        "w_in": nrm(ks[5], (DEPTH, D_MODEL, D_IN), D_MODEL ** -0.5),
        "w_pool": nrm(ks[6], (DEPTH, N_POOL_GROUPS, POOL_GROUP, POOL_GROUP), POOL_GROUP ** -0.5),
        "pool_scale": 1.0 + nrm(ks[7], (DEPTH, D_POOL), 0.1),
        "conv_w": nrm(ks[8], (DEPTH, CONV_WIDTH, D_CONV), CONV_WIDTH ** -0.5),
        "w_out": nrm(ks[9], (DEPTH, D_MIX, D_MODEL), D_MIX ** -0.5),
        "g_post": 1.0 + nrm(ks[10], (DEPTH, D_MODEL), 0.02),
        "w_ple": nrm(ks[11], (DEPTH, D_PLE, D_MODEL), D_PLE ** -0.5),
        "w_ple_gate": nrm(ks[12], (DEPTH, D_MODEL, D_MODEL), D_MODEL ** -0.5),
        "g_ple": 1.0 + nrm(ks[13], (DEPTH, D_MODEL), 0.02),
    }


def reference(x_prompt, x_sample, p_prompt, p_sample, g_pre, w_in, w_pool, pool_scale, conv_w,
              w_out, g_post, w_ple, w_ple_gate, g_ple):
    y_prompt = _trunk(x_prompt, p_prompt, g_pre, w_in, w_pool, pool_scale, conv_w, w_out,
                      g_post, w_ple, w_ple_gate, g_ple)
    y_sample = _trunk(x_sample, p_sample, g_pre, w_in, w_pool, pool_scale, conv_w, w_out,
                      g_post, w_ple, w_ple_gate, g_ple)
    return (y_prompt, y_sample)
```

```python
from functools import partial

import jax
import jax.numpy as jnp
from jax.experimental import pallas as pl
from jax.experimental.pallas import tpu as pltpu

D_MODEL = 1024
D_POOL = 512
D_CONV = 512
POOL_WINDOWS = (2, 4, 8, 16)
POOL_GROUP = 128
D_PLE = 256
EPS = 1e-6

HALO = 16
SEQ_TILE = 512
VMEM_LIMIT_BYTES = 56 * 1024 * 1024

_A_VAL, _A_GATE, _B_U, _B_B, _B_C, _B_GATE = (i * D_POOL for i in range(6))


def _rms_scale(v):
    return jax.lax.rsqrt(jnp.mean(v * v, axis=-1, keepdims=True) + EPS)


def _dot(a, b):
    return jnp.dot(a, b, preferred_element_type=jnp.float32)


def _layer_kernel(xp_ref, x_ref, xn_ref, p_ref, gpre_ref, win_ref, wpool_ref,
                  pscale_ref, convw_ref, wout_ref, gpost_ref, wple_ref,
                  wgate_ref, gple_ref, out_ref, h_scr, mix_scr, inv_scr, *,
                  seq_len, tile):
    s = pl.program_id(1)
    n_tiles = seq_len // tile
    bf16 = jnp.bfloat16

    def fill_inv_count():
        t = s * tile + jax.lax.broadcasted_iota(jnp.int32, (tile, POOL_GROUP), 0)
        for g, w in enumerate(POOL_WINDOWS):
            left, right = (w - 1) // 2, w // 2
            lo = jnp.maximum(t - left, 0)
            hi = jnp.minimum(t + right + 1, seq_len)
            inv_scr[g] = 1.0 / (hi - lo).astype(jnp.float32)

    pl.when((s <= 1) | (s == n_tiles - 1))(fill_inv_count)

    g_pre = gpre_ref[...]

    def pre_norm(v):
        return (v * _rms_scale(v) * g_pre).astype(bf16)

    h_prev = pre_norm(xp_ref[0])
    h_next = pre_norm(xn_ref[0])
    h_scr[0:HALO, :] = jnp.where(s > 0, h_prev, jnp.zeros_like(h_prev))
    h_scr[HALO:HALO + tile, :] = pre_norm(x_ref[0])
    h_scr[HALO + tile:, :] = jnp.where(s < n_tiles - 1, h_next, jnp.zeros_like(h_next))

    h_all = h_scr[...]
    h_mid = h_scr[HALO:HALO + tile, :]
    mix_scr[:, 0:D_POOL] = _dot(h_all, win_ref[:, _A_VAL:_A_VAL + D_POOL])
    mix_scr[:, D_POOL:] = (_dot(h_all, win_ref[:, _B_C:_B_C + D_CONV])
                           * _dot(h_all, win_ref[:, _B_U:_B_U + D_CONV]))
    a_gate = _dot(h_mid, win_ref[:, _A_GATE:_A_GATE + D_POOL])
    b_b = _dot(h_mid, win_ref[:, _B_B:_B_B + D_CONV])
    b_gate = _dot(h_mid, win_ref[:, _B_GATE:_B_GATE + D_CONV])

    diffs = []
    for g, w in enumerate(POOL_WINDOWS):
        left, right = (w - 1) // 2, w // 2
        cols = slice(g * POOL_GROUP, (g + 1) * POOL_GROUP)
        total = mix_scr[pl.ds(HALO - left, tile), cols]
        for k in range(-left + 1, right + 1):
            total = total + mix_scr[pl.ds(HALO + k, tile), cols]
        diffs.append(total * inv_scr[g] - mix_scr[pl.ds(HALO, tile), cols])
    d = jnp.concatenate(diffs, axis=-1).astype(bf16)
    half = 2 * POOL_GROUP
    pooled = jnp.concatenate(
        [_dot(d[:, 0:half], wpool_ref[0]), _dot(d[:, half:], wpool_ref[1])], axis=-1)
    ya = pooled * pscale_ref[...] * jax.nn.silu(a_gate)

    conv_cols = slice(D_POOL, D_POOL + D_CONV)
    conv = (mix_scr[pl.ds(HALO - 1, tile), conv_cols] * convw_ref[0:1, :]
            + mix_scr[pl.ds(HALO, tile), conv_cols] * convw_ref[1:2, :]
            + mix_scr[pl.ds(HALO + 1, tile), conv_cols] * convw_ref[2:3, :])
    yb = b_b * conv * jax.nn.silu(b_gate)

    o = _dot(jnp.concatenate([ya, yb], axis=-1).astype(bf16), wout_ref[...])
    x1 = x_ref[0] + o * _rms_scale(o) * gpost_ref[...]

    gate = jax.nn.sigmoid(_dot(x1.astype(bf16), wgate_ref[...]))
    e = _dot(p_ref[0].astype(bf16), wple_ref[...]) * gate
    out_ref[0] = x1 + e * _rms_scale(e) * gple_ref[...]


def _resident(shape):
    return pl.BlockSpec(shape, lambda b, s: (0,) * len(shape),
                        pipeline_mode=pl.Buffered(1))


def _layer(x, p, g_pre, w_in, w_pool2, pool_scale, conv_w, w_out, g_post,
           w_ple, w_gate, g_ple, *, tile=SEQ_TILE):
    batch, seq_len, _ = x.shape
    assert seq_len % tile == 0 and tile % HALO == 0
    n_tiles = seq_len // tile
    halo_per_tile = tile // HALO
    n_halo_blocks = seq_len // HALO

    def prev_map(b, s):
        return (b, jnp.maximum(s * halo_per_tile - 1, 0), 0)

    def next_map(b, s):
        return (b, jnp.minimum((s + 1) * halo_per_tile, n_halo_blocks - 1), 0)

    kernel_fn = partial(_layer_kernel, seq_len=seq_len, tile=tile)
    return pl.pallas_call(
        kernel_fn,
        grid=(batch, n_tiles),
        in_specs=[
            pl.BlockSpec((1, HALO, D_MODEL), prev_map),
            pl.BlockSpec((1, tile, D_MODEL), lambda b, s: (b, s, 0)),
            pl.BlockSpec((1, HALO, D_MODEL), next_map),
            pl.BlockSpec((1, tile, D_PLE), lambda b, s: (b, s, 0)),
            _resident(g_pre.shape),
            _resident(w_in.shape),
            _resident(w_pool2.shape),
            _resident(pool_scale.shape),
            _resident(conv_w.shape),
            _resident(w_out.shape),
            _resident(g_post.shape),
            _resident(w_ple.shape),
            _resident(w_gate.shape),
            _resident(g_ple.shape),
        ],
        out_specs=pl.BlockSpec((1, tile, D_MODEL), lambda b, s: (b, s, 0)),
        out_shape=jax.ShapeDtypeStruct(x.shape, x.dtype),
        scratch_shapes=[
            pltpu.VMEM((tile + 2 * HALO, D_MODEL), jnp.bfloat16),
            pltpu.VMEM((tile + 2 * HALO, D_POOL + D_CONV), jnp.float32),
            pltpu.VMEM((len(POOL_WINDOWS), tile, POOL_GROUP), jnp.float32),
        ],
        compiler_params=pltpu.CompilerParams(
            dimension_semantics=("arbitrary", "arbitrary"),
            vmem_limit_bytes=VMEM_LIMIT_BYTES),
        name="encoder_layer",
    )(x, x, x, p, g_pre, w_in, w_pool2, pool_scale, conv_w, w_out, g_post,
      w_ple, w_gate, g_ple)


def _pair_block_diag(w_pool):
    z = jnp.zeros_like(w_pool[0])
    return jnp.stack([jnp.block([[w_pool[0], z], [z, w_pool[1]]]),
                      jnp.block([[w_pool[2], z], [z, w_pool[3]]])])


def kernel(x_prompt, x_sample, p_prompt, p_sample, g_pre, w_in, w_pool, pool_scale, conv_w, w_out, g_post, w_ple, w_ple_gate, g_ple):
    bf16 = jnp.bfloat16
    depth = g_pre.shape[0]
    xs = [x_prompt, x_sample]
    for i in range(depth):
        weights = (
            g_pre[i][None, :], w_in[i].astype(bf16),
            _pair_block_diag(w_pool[i]).astype(bf16), pool_scale[i][None, :],
            conv_w[i], w_out[i].astype(bf16), g_post[i][None, :],
            w_ple[i].astype(bf16), w_ple_gate[i].astype(bf16), g_ple[i][None, :])
        xs = [_layer(x, p[i], *weights) for x, p in zip(xs, (p_prompt, p_sample))]
    return tuple(xs)
```

```python
from functools import partial

import jax
import jax.numpy as jnp
from jax.experimental import pallas as pl
from jax.experimental.pallas import tpu as pltpu

D_MODEL = 1024
D_POOL = 512
D_CONV = 512
POOL_WINDOWS = (2, 4, 8, 16)
LANES = 128
SUBLANES = 8
D_PLE = 256
EPS = 1e-6

HALO = 16
TILE = 512
TILES_PER_STEP = 2
VMEM_LIMIT_BYTES = 58 * 1024 * 1024

_A_VAL, _A_GATE, _B_U, _B_B, _B_C, _B_GATE = (i * D_POOL for i in range(6))
_N_POOL_BLOCKS = D_POOL // LANES
_N_CONV_BLOCKS = D_CONV // LANES


def _rms_scale(v):
    return jax.lax.rsqrt(jnp.mean(v * v, axis=-1, keepdims=True) + EPS)


_dot = partial(jnp.dot, preferred_element_type=jnp.float32)


def _pre_norm_stage(first_tile, xprev_ref, xin_ref, xnext_ref, gpre_ref, h_scr,
                    *, tiles_per_seq):
    g_pre = gpre_ref[...]

    def pre_norm(v):
        return (v * _rms_scale(v) * g_pre).astype(jnp.bfloat16)

    starts_seq = first_tile % tiles_per_seq == 0
    ends_seq = (first_tile + 1) % tiles_per_seq == tiles_per_seq - 1
    h_prev = pre_norm(xprev_ref[...])
    h_next = pre_norm(xnext_ref[...])
    lo = pre_norm(xin_ref[0:TILE, :])
    hi = pre_norm(xin_ref[TILE:2 * TILE, :])
    h_scr[0][0:HALO, :] = jnp.where(starts_seq, jnp.zeros_like(h_prev), h_prev)
    h_scr[0][HALO:HALO + TILE, :] = lo
    h_scr[0][HALO + TILE:, :] = hi[0:HALO]
    h_scr[1][0:HALO, :] = lo[TILE - HALO:]
    h_scr[1][HALO:HALO + TILE, :] = hi
    h_scr[1][HALO + TILE:, :] = jnp.where(ends_seq, jnp.zeros_like(h_next), h_next)


def _window_counts(row0, left, right, seq_len):
    pos = row0 + jax.lax.broadcasted_iota(jnp.int32, (SUBLANES, LANES), 0)
    lo = jnp.maximum(pos - left, 0)
    hi = jnp.minimum(pos + right + 1, seq_len)
    return (hi - lo).astype(jnp.float32)


def _mixer_stage(slot, tile_idx, win_ref, wpool_ref, pscale_ref, convw_ref,
                 h_scr, mix_scr, *, seq_len):
    bf16 = jnp.bfloat16
    tiles_per_seq = seq_len // TILE
    seq_row0 = (tile_idx % tiles_per_seq) * TILE

    h_all = h_scr[slot][...]
    h_mid = h_scr[slot][HALO:HALO + TILE, :]
    a_val = _dot(h_all, win_ref[:, _A_VAL:_A_VAL + D_POOL])
    for j in range(_N_POOL_BLOCKS):
        mix_scr[slot][j] = a_val[:, j * LANES:(j + 1) * LANES]
    cu = (_dot(h_all, win_ref[:, _B_C:_B_C + D_CONV])
          * _dot(h_all, win_ref[:, _B_U:_B_U + D_CONV]))
    for j in range(_N_CONV_BLOCKS):
        mix_scr[slot][_N_POOL_BLOCKS + j] = cu[:, j * LANES:(j + 1) * LANES]
    a_gate = _dot(h_mid, win_ref[:, _A_GATE:_A_GATE + D_POOL])
    b_b = _dot(h_mid, win_ref[:, _B_B:_B_B + D_CONV])
    b_gate = _dot(h_mid, win_ref[:, _B_GATE:_B_GATE + D_CONV])

    def rows(block, shift):
        return mix_scr[slot][block, pl.ds(HALO + shift, TILE), :]

    diffs = []
    for g, w in enumerate(POOL_WINDOWS):
        left, right = (w - 1) // 2, w // 2
        total = rows(g, -left)
        for k in range(-left + 1, right + 1):
            total = total + rows(g, k)
        centre = rows(g, 0)
        top = slice(0, SUBLANES)
        mid = slice(SUBLANES, TILE - SUBLANES)
        bot = slice(TILE - SUBLANES, TILE)
        diffs.append(jnp.concatenate([
            total[top] / _window_counts(seq_row0, left, right, seq_len) - centre[top],
            total[mid] * (1.0 / w) - centre[mid],
            total[bot] / _window_counts(seq_row0 + TILE - SUBLANES, left, right, seq_len)
            - centre[bot],
        ], axis=0))
    d = jnp.concatenate(diffs, axis=-1).astype(bf16)
    half = D_POOL // 2
    pooled = jnp.concatenate(
        [_dot(d[:, 0:half], wpool_ref[0]), _dot(d[:, half:], wpool_ref[1])], axis=-1)
    ya = pooled * pscale_ref[...] * jax.nn.silu(a_gate)

    conv = jnp.concatenate(
        [rows(_N_POOL_BLOCKS + j, -1) * convw_ref[0:1, j * LANES:(j + 1) * LANES]
         + rows(_N_POOL_BLOCKS + j, 0) * convw_ref[1:2, j * LANES:(j + 1) * LANES]
         + rows(_N_POOL_BLOCKS + j, 1) * convw_ref[2:3, j * LANES:(j + 1) * LANES]
         for j in range(_N_CONV_BLOCKS)], axis=-1)
    yb = b_b * conv * jax.nn.silu(b_gate)

    return jnp.concatenate([ya, yb], axis=-1).astype(bf16)


def _output_stage(slot, x_rows, p_rows, gpost_ref, wple_ref, wgate_ref, gple_ref, o_scr):
    bf16 = jnp.bfloat16
    o = o_scr[slot][...]
    x1 = x_rows + o * _rms_scale(o) * gpost_ref[...]
    gate = jax.nn.sigmoid(_dot(x1.astype(bf16), wgate_ref[...]))
    e = _dot(p_rows.astype(bf16), wple_ref[...]) * gate
    return x1 + e * _rms_scale(e) * gple_ref[...]


def _layer_kernel(xprev_ref, xin_ref, xnext_ref, xres_ref, p_ref, gpre_ref,
                  win_ref, wpool_ref, pscale_ref, convw_ref, wout_ref, gpost_ref,
                  wple_ref, wgate_ref, gple_ref, out_ref, h0_scr, h1_scr, mix0_scr,
                  mix1_scr, o0_scr, o1_scr, *, seq_len):
    j = pl.program_id(0)
    tiles_per_seq = seq_len // TILE
    h_scr, mix_scr, o_scr = (h0_scr, h1_scr), (mix0_scr, mix1_scr), (o0_scr, o1_scr)

    @pl.when(j == 0)
    def _():
        for ref in h_scr + o_scr:
            ref[...] = jnp.zeros_like(ref)

    mixer = partial(_mixer_stage, win_ref=win_ref, wpool_ref=wpool_ref,
                    pscale_ref=pscale_ref, convw_ref=convw_ref,
                    h_scr=h_scr, mix_scr=mix_scr, seq_len=seq_len)
    output = partial(_output_stage, gpost_ref=gpost_ref, wple_ref=wple_ref,
                     wgate_ref=wgate_ref, gple_ref=gple_ref, o_scr=o_scr)

    y1 = mixer(1, 2 * j - 1)
    out_ref[0:TILE, :] = output(0, xres_ref[0:TILE, :], p_ref[0:TILE, :])
    o_scr[1][...] = _dot(y1, wout_ref[...])
    _pre_norm_stage(2 * j, xprev_ref, xin_ref, xnext_ref, gpre_ref, h_scr,
                    tiles_per_seq=tiles_per_seq)
    y0 = mixer(0, 2 * j)
    out_ref[TILE:, :] = output(1, xres_ref[TILE:, :], p_ref[TILE:, :])
    o_scr[0][...] = _dot(y0, wout_ref[...])


def _resident(shape):
    return pl.BlockSpec(shape, lambda j: (0,) * len(shape),
                        pipeline_mode=pl.Buffered(1))


def _layer(x, p, g_pre, w_in, w_pool2, pool_scale, conv_w, w_out, g_post,
           w_ple, w_gate, g_ple):
    batch, seq_len, _ = x.shape
    step_rows = TILES_PER_STEP * TILE
    assert TILES_PER_STEP == 2 and seq_len % step_rows == 0 and TILE % HALO == 0
    n_rows = batch * seq_len
    n_blocks = n_rows // step_rows
    n_halo_blocks = n_rows // HALO
    halo_per_step = step_rows // HALO
    x2 = x.reshape(n_rows, D_MODEL)
    p2 = p.reshape(n_rows, D_PLE)

    def in_block(j):
        return (jnp.minimum(j, n_blocks - 1), 0)

    def out_block(j):
        return (jnp.maximum(j - 1, 0), 0)

    def prev_halo(j):
        return (jnp.clip(j * halo_per_step - 1, 0, n_halo_blocks - 1), 0)

    def next_halo(j):
        return (jnp.minimum((j + 1) * halo_per_step, n_halo_blocks - 1), 0)

    kernel_fn = partial(_layer_kernel, seq_len=seq_len)
    mix_shape = (_N_POOL_BLOCKS + _N_CONV_BLOCKS, TILE + 2 * HALO, LANES)
    out = pl.pallas_call(
        kernel_fn,
        grid=(n_blocks + 1,),
        in_specs=[
            pl.BlockSpec((HALO, D_MODEL), prev_halo),
            pl.BlockSpec((step_rows, D_MODEL), in_block),
            pl.BlockSpec((HALO, D_MODEL), next_halo),
            pl.BlockSpec((step_rows, D_MODEL), out_block),
            pl.BlockSpec((step_rows, D_PLE), out_block),
            _resident(g_pre.shape),
            _resident(w_in.shape),
            _resident(w_pool2.shape),
            _resident(pool_scale.shape),
            _resident(conv_w.shape),
            _resident(w_out.shape),
            _resident(g_post.shape),
            _resident(w_ple.shape),
            _resident(w_gate.shape),
            _resident(g_ple.shape),
        ],
        out_specs=pl.BlockSpec((step_rows, D_MODEL), out_block),
        out_shape=jax.ShapeDtypeStruct((n_rows, D_MODEL), x.dtype),
        scratch_shapes=[
            pltpu.VMEM((TILE + 2 * HALO, D_MODEL), jnp.bfloat16),
            pltpu.VMEM((TILE + 2 * HALO, D_MODEL), jnp.bfloat16),
            pltpu.VMEM(mix_shape, jnp.float32),
            pltpu.VMEM(mix_shape, jnp.float32),
            pltpu.VMEM((TILE, D_MODEL), jnp.float32),
            pltpu.VMEM((TILE, D_MODEL), jnp.float32),
        ],
        compiler_params=pltpu.CompilerParams(
            dimension_semantics=("arbitrary",),
            vmem_limit_bytes=VMEM_LIMIT_BYTES),
        name="encoder_layer",
    )(x2, x2, x2, x2, p2, g_pre, w_in, w_pool2, pool_scale, conv_w, w_out,
      g_post, w_ple, w_gate, g_ple)
    return out.reshape(x.shape)


def _pair_block_diag(w_pool):
    z = jnp.zeros_like(w_pool[0])
    return jnp.stack([jnp.block([[w_pool[0], z], [z, w_pool[1]]]),
                      jnp.block([[w_pool[2], z], [z, w_pool[3]]])])


def kernel(x_prompt, x_sample, p_prompt, p_sample, g_pre, w_in, w_pool, pool_scale, conv_w, w_out, g_post, w_ple, w_ple_gate, g_ple):
    bf16 = jnp.bfloat16
    depth = g_pre.shape[0]
    xs = [x_prompt, x_sample]
    for i in range(depth):
        weights = (
            g_pre[i][None, :], w_in[i].astype(bf16),
            _pair_block_diag(w_pool[i]).astype(bf16), pool_scale[i][None, :],
            conv_w[i], w_out[i].astype(bf16), g_post[i][None, :],
            w_ple[i].astype(bf16), w_ple_gate[i].astype(bf16), g_ple[i][None, :])
        xs = [_layer(x, p[i], *weights) for x, p in zip(xs, (p_prompt, p_sample))]
    return tuple(xs)
```

```python
from functools import partial

import jax
import jax.numpy as jnp
from jax.experimental import pallas as pl
from jax.experimental.pallas import tpu as pltpu

D_MODEL = 1024
D_POOL = 512
D_CONV = 512
POOL_WINDOWS = (2, 4, 8, 16)
LANES = 128
SUBLANES = 8
D_PLE = 256
EPS = 1e-6

HALO = 16
TILE = 512
ROW_CHUNK = 128
VMEM_LIMIT_BYTES = 58 * 1024 * 1024

_A_VAL, _A_GATE, _B_U, _B_B, _B_C, _B_GATE = (i * D_POOL for i in range(6))
_PAIR = 256
_N_PAIRS = D_CONV // _PAIR
_CU_PAIRS = 2 * D_POOL
_BG_PAIRS = _CU_PAIRS + 2 * D_CONV
_N_POOL_BLOCKS = D_POOL // LANES
_N_CONV_BLOCKS = D_CONV // LANES


def _rms_scale(v):
    return jax.lax.rsqrt(jnp.mean(v * v, axis=-1, keepdims=True) + EPS)


_dot = partial(jnp.dot, preferred_element_type=jnp.float32)


def _pre_norm_stage(s, xprev_ref, x_ref, xnext_ref, gpre_ref, h_scr, *, tiles_per_seq):
    g_pre = gpre_ref[...]

    def pre_norm(v):
        return (v * _rms_scale(v) * g_pre).astype(jnp.bfloat16)

    before = pre_norm(xprev_ref[0])
    after = pre_norm(xnext_ref[0])
    h_scr[0:HALO, :] = jnp.where(s == 0, jnp.zeros_like(before), before)
    h_scr[HALO:HALO + TILE, :] = pre_norm(x_ref[0])
    h_scr[HALO + TILE:, :] = jnp.where(s == tiles_per_seq - 1, jnp.zeros_like(after), after)


def _window_counts(row0, left, right, seq_len):
    pos = row0 + jax.lax.broadcasted_iota(jnp.int32, (SUBLANES, LANES), 0)
    lo = jnp.maximum(pos - left, 0)
    hi = jnp.minimum(pos + right + 1, seq_len)
    return (hi - lo).astype(jnp.float32)


def _project_values(win_ref, h_scr, mix_scr):
    a_val = _dot(h_scr[...], win_ref[:, _A_VAL:_A_VAL + D_POOL])
    for j in range(_N_POOL_BLOCKS):
        mix_scr[j] = a_val[:, j * LANES:(j + 1) * LANES]


def _project_conv_inputs(win_ref, h_scr, mix_scr):
    blocks_per_pair = _PAIR // LANES
    h_all = h_scr[...]
    for i in range(_N_PAIRS):
        pair = _dot(h_all, win_ref[:, _CU_PAIRS + 2 * i * _PAIR:_CU_PAIRS + 2 * (i + 1) * _PAIR])
        cu = pair[:, 0:_PAIR] * pair[:, _PAIR:]
        for j in range(blocks_per_pair):
            mix_scr[_N_POOL_BLOCKS + i * blocks_per_pair + j] = (
                cu[:, j * LANES:(j + 1) * LANES])


def _project_gates(win_ref, pscale_ref, h_scr):
    h_mid = h_scr[HALO:HALO + TILE, :]
    a_scale = pscale_ref[...] * jax.nn.silu(
        _dot(h_mid, win_ref[:, _A_GATE:_A_GATE + D_POOL]))
    b_scales = []
    for i in range(_N_PAIRS):
        pair = _dot(h_mid, win_ref[:, _BG_PAIRS + 2 * i * _PAIR:_BG_PAIRS + 2 * (i + 1) * _PAIR])
        b_scales.append(pair[:, 0:_PAIR] * jax.nn.silu(pair[:, _PAIR:]))
    return a_scale, jnp.concatenate(b_scales, axis=-1)


def _shifted_rows(mix_scr, block, first, chunk_row0, n_rows):
    return mix_scr[block, pl.ds(HALO + first + chunk_row0, n_rows), :]


def _up(v, k):
    return pltpu.roll(v, v.shape[0] - k, axis=0)


def _pool_mixer(s, wpool_ref, mix_scr, *, seq_len):
    seq_row0 = s * TILE
    top = slice(0, SUBLANES)
    mid = slice(SUBLANES, TILE - SUBLANES)
    bot = slice(TILE - SUBLANES, TILE)
    diffs = []
    for g, w in enumerate(POOL_WINDOWS):
        left, right = (w - 1) // 2, w // 2
        pad = SUBLANES * pl.cdiv(w - 1, SUBLANES)
        sums = []
        for r0 in range(0, TILE, ROW_CHUNK):
            f = _shifted_rows(mix_scr, g, -left, r0, ROW_CHUNK + pad)
            span = 1
            while span < w:
                f = f + _up(f, span)
                span *= 2
            sums.append(f[0:ROW_CHUNK])
        total = jnp.concatenate(sums, axis=0)
        centre = _shifted_rows(mix_scr, g, 0, 0, TILE)
        diffs.append(jnp.concatenate([
            total[top] / _window_counts(seq_row0, left, right, seq_len) - centre[top],
            total[mid] * (1.0 / w) - centre[mid],
            total[bot] / _window_counts(seq_row0 + TILE - SUBLANES, left, right, seq_len)
            - centre[bot],
        ], axis=0))
    d = jnp.concatenate(diffs, axis=-1).astype(jnp.bfloat16)
    half = D_POOL // 2
    return jnp.concatenate(
        [_dot(d[:, 0:half], wpool_ref[0]), _dot(d[:, half:], wpool_ref[1])], axis=-1)


def _conv_mixer(convw_ref, mix_scr):
    conv_blocks = []
    for j in range(_N_CONV_BLOCKS):
        taps = [convw_ref[k:k + 1, j * LANES:(j + 1) * LANES] for k in range(3)]
        chunks = []
        for r0 in range(0, TILE, ROW_CHUNK):
            f = _shifted_rows(mix_scr, _N_POOL_BLOCKS + j, -1, r0, ROW_CHUNK + SUBLANES)
            chunks.append((f * taps[0] + _up(f, 1) * taps[1] + _up(f, 2) * taps[2])[0:ROW_CHUNK])
        conv_blocks.append(jnp.concatenate(chunks, axis=0))
    return jnp.concatenate(conv_blocks, axis=-1)


def _layer_kernel(xprev_ref, x_ref, xnext_ref, p_ref, gpre_ref, win_ref, wpool_ref,
                  pscale_ref, convw_ref, wout_ref, gpost_ref, wple_ref, wgate_ref,
                  gple_ref, out_ref, h_scr, mix_scr, *, seq_len):
    s = pl.program_id(1)
    bf16 = jnp.bfloat16
    _pre_norm_stage(s, xprev_ref, x_ref, xnext_ref, gpre_ref, h_scr,
                    tiles_per_seq=seq_len // TILE)

    _project_values(win_ref, h_scr, mix_scr)
    _project_conv_inputs(win_ref, h_scr, mix_scr)
    a_scale, b_scale = _project_gates(win_ref, pscale_ref, h_scr)
    ya = _pool_mixer(s, wpool_ref, mix_scr, seq_len=seq_len) * a_scale
    yb = _conv_mixer(convw_ref, mix_scr) * b_scale

    o = _dot(jnp.concatenate([ya, yb], axis=-1).astype(bf16), wout_ref[...])
    x1 = x_ref[0] + o * _rms_scale(o) * gpost_ref[...]

    gate = jax.nn.sigmoid(_dot(x1.astype(bf16), wgate_ref[...]))
    e = _dot(p_ref[0].astype(bf16), wple_ref[...]) * gate
    out_ref[0] = x1 + e * _rms_scale(e) * gple_ref[...]


def _resident(shape):
    return pl.BlockSpec(shape, lambda b, s: (0,) * len(shape),
                        pipeline_mode=pl.Buffered(1))


def _layer(x, p, g_pre, w_in, w_pool2, pool_scale, conv_w, w_out, g_post,
           w_ple, w_gate, g_ple):
    batch, seq_len, _ = x.shape
    assert seq_len % TILE == 0 and TILE % HALO == 0 and TILE % ROW_CHUNK == 0
    n_tiles = seq_len // TILE
    halo_per_tile = TILE // HALO
    n_halo_blocks = seq_len // HALO

    def prev_halo(b, s):
        return (b, jnp.maximum(s * halo_per_tile - 1, 0), 0)

    def next_halo(b, s):
        return (b, jnp.minimum((s + 1) * halo_per_tile, n_halo_blocks - 1), 0)

    def tile_block(b, s):
        return (b, s, 0)

    return pl.pallas_call(
        partial(_layer_kernel, seq_len=seq_len),
        grid=(batch, n_tiles),
        in_specs=[
            pl.BlockSpec((1, HALO, D_MODEL), prev_halo),
            pl.BlockSpec((1, TILE, D_MODEL), tile_block),
            pl.BlockSpec((1, HALO, D_MODEL), next_halo),
            pl.BlockSpec((1, TILE, D_PLE), tile_block),
            _resident(g_pre.shape),
            _resident(w_in.shape),
            _resident(w_pool2.shape),
            _resident(pool_scale.shape),
            _resident(conv_w.shape),
            _resident(w_out.shape),
            _resident(g_post.shape),
            _resident(w_ple.shape),
            _resident(w_gate.shape),
            _resident(g_ple.shape),
        ],
        out_specs=pl.BlockSpec((1, TILE, D_MODEL), tile_block),
        out_shape=jax.ShapeDtypeStruct(x.shape, x.dtype),
        scratch_shapes=[
            pltpu.VMEM((TILE + 2 * HALO, D_MODEL), jnp.bfloat16),
            pltpu.VMEM((_N_POOL_BLOCKS + _N_CONV_BLOCKS, TILE + 2 * HALO, LANES), jnp.float32),
        ],
        compiler_params=pltpu.CompilerParams(
            dimension_semantics=("arbitrary", "arbitrary"),
            vmem_limit_bytes=VMEM_LIMIT_BYTES),
        name="encoder_layer",
    )(x, x, x, p, g_pre, w_in, w_pool2, pool_scale, conv_w, w_out, g_post,
      w_ple, w_gate, g_ple)


def _interleave_pairs(w_in):
    def blocks(start):
        return [w_in[:, start + i * _PAIR:start + (i + 1) * _PAIR] for i in range(_N_PAIRS)]

    cols = [w_in[:, _A_VAL:_A_VAL + D_POOL], w_in[:, _A_GATE:_A_GATE + D_POOL]]
    for first, second in ((_B_C, _B_U), (_B_B, _B_GATE)):
        for a, b in zip(blocks(first), blocks(second)):
            cols += [a, b]
    return jnp.concatenate(cols, axis=1)


def _pair_block_diag(w_pool):
    z = jnp.zeros_like(w_pool[0])
    return jnp.stack([jnp.block([[w_pool[0], z], [z, w_pool[1]]]),
                      jnp.block([[w_pool[2], z], [z, w_pool[3]]])])


def kernel(x_prompt, x_sample, p_prompt, p_sample, g_pre, w_in, w_pool, pool_scale, conv_w, w_out, g_post, w_ple, w_ple_gate, g_ple):
    bf16 = jnp.bfloat16
    depth = g_pre.shape[0]
    xs = [x_prompt, x_sample]
    for i in range(depth):
        weights = (
            g_pre[i][None, :], _interleave_pairs(w_in[i]).astype(bf16),
            _pair_block_diag(w_pool[i]).astype(bf16), pool_scale[i][None, :],
            conv_w[i], w_out[i].astype(bf16), g_post[i][None, :],
            w_ple[i].astype(bf16), w_ple_gate[i].astype(bf16), g_ple[i][None, :])
        xs = [_layer(x, p[i], *weights) for x, p in zip(xs, (p_prompt, p_sample))]
    return tuple(xs)
```

```python
from functools import partial

import jax
import jax.numpy as jnp
from jax.experimental import pallas as pl
from jax.experimental.pallas import tpu as pltpu

D_MODEL = 1024
D_POOL = 512
D_CONV = 512
POOL_WINDOWS = (2, 4, 8, 16)
LANES = 128
SUBLANES = 8
D_PLE = 256
EPS = 1e-6

HALO = 16
TILE = 1024
ROW_CHUNK = 128
VMEM_LIMIT_BYTES = 58 * 1024 * 1024

_A_VAL, _A_GATE, _B_U, _B_B, _B_C, _B_GATE = (i * D_POOL for i in range(6))
_PAIR = 256
_N_PAIRS = D_CONV // _PAIR
_CU_PAIRS = 2 * D_POOL
_BG_PAIRS = _CU_PAIRS + 2 * D_CONV
_N_POOL_BLOCKS = D_POOL // LANES
_N_CONV_BLOCKS = D_CONV // LANES


def _rms_scale(v):
    return jax.lax.rsqrt(jnp.mean(v * v, axis=-1, keepdims=True) + EPS)


_dot = partial(jnp.dot, preferred_element_type=jnp.float32)


def _pre_norm_stage(s, xprev_ref, x_ref, xnext_ref, gpre_ref, h_scr, *, tiles_per_seq):
    g_pre = gpre_ref[...]

    def pre_norm(v):
        return (v * _rms_scale(v) * g_pre).astype(jnp.bfloat16)

    before = pre_norm(xprev_ref[0])
    after = pre_norm(xnext_ref[0])
    h_scr[0:HALO, :] = jnp.where(s == 0, jnp.zeros_like(before), before)
    h_scr[HALO:HALO + TILE, :] = pre_norm(x_ref[0])
    h_scr[HALO + TILE:, :] = jnp.where(s == tiles_per_seq - 1, jnp.zeros_like(after), after)


def _window_counts(row0, left, right, seq_len):
    pos = row0 + jax.lax.broadcasted_iota(jnp.int32, (SUBLANES, LANES), 0)
    lo = jnp.maximum(pos - left, 0)
    hi = jnp.minimum(pos + right + 1, seq_len)
    return (hi - lo).astype(jnp.float32)


def _project_values(win_ref, h_scr, mix_scr):
    a_val = _dot(h_scr[...], win_ref[:, _A_VAL:_A_VAL + D_POOL])
    for j in range(_N_POOL_BLOCKS):
        mix_scr[j] = a_val[:, j * LANES:(j + 1) * LANES]


def _project_conv_inputs(win_ref, h_scr, mix_scr):
    blocks_per_pair = _PAIR // LANES
    h_all = h_scr[...]
    for i in range(_N_PAIRS):
        pair = _dot(h_all, win_ref[:, _CU_PAIRS + 2 * i * _PAIR:_CU_PAIRS + 2 * (i + 1) * _PAIR])
        cu = pair[:, 0:_PAIR] * pair[:, _PAIR:]
        for j in range(blocks_per_pair):
            mix_scr[_N_POOL_BLOCKS + i * blocks_per_pair + j] = (
                cu[:, j * LANES:(j + 1) * LANES])


def _project_gates(win_ref, pscale_ref, h_scr):
    h_mid = h_scr[HALO:HALO + TILE, :]
    a_scale = pscale_ref[...] * jax.nn.silu(
        _dot(h_mid, win_ref[:, _A_GATE:_A_GATE + D_POOL]))
    b_scales = []
    for i in range(_N_PAIRS):
        pair = _dot(h_mid, win_ref[:, _BG_PAIRS + 2 * i * _PAIR:_BG_PAIRS + 2 * (i + 1) * _PAIR])
        b_scales.append(pair[:, 0:_PAIR] * jax.nn.silu(pair[:, _PAIR:]))
    return a_scale, jnp.concatenate(b_scales, axis=-1)


def _shifted_rows(mix_scr, block, first, chunk_row0, n_rows):
    return mix_scr[block, pl.ds(HALO + first + chunk_row0, n_rows), :]


def _up(v, k):
    return pltpu.roll(v, v.shape[0] - k, axis=0)


def _pool_mixer(s, wpool_ref, mix_scr, *, seq_len):
    seq_row0 = s * TILE
    top = slice(0, SUBLANES)
    mid = slice(SUBLANES, TILE - SUBLANES)
    bot = slice(TILE - SUBLANES, TILE)
    diffs = []
    for g, w in enumerate(POOL_WINDOWS):
        left, right = (w - 1) // 2, w // 2
        pad = SUBLANES * pl.cdiv(w - 1, SUBLANES)
        sums = []
        for r0 in range(0, TILE, ROW_CHUNK):
            f = _shifted_rows(mix_scr, g, -left, r0, ROW_CHUNK + pad)
            span = 1
            while span < w:
                f = f + _up(f, span)
                span *= 2
            sums.append(f[0:ROW_CHUNK])
        total = jnp.concatenate(sums, axis=0)
        centre = _shifted_rows(mix_scr, g, 0, 0, TILE)
        diffs.append(jnp.concatenate([
            total[top] / _window_counts(seq_row0, left, right, seq_len) - centre[top],
            total[mid] * (1.0 / w) - centre[mid],
            total[bot] / _window_counts(seq_row0 + TILE - SUBLANES, left, right, seq_len)
            - centre[bot],
        ], axis=0))
    d = jnp.concatenate(diffs, axis=-1).astype(jnp.bfloat16)
    half = D_POOL // 2
    return jnp.concatenate(
        [_dot(d[:, 0:half], wpool_ref[0]), _dot(d[:, half:], wpool_ref[1])], axis=-1)


def _conv_mixer(convw_ref, mix_scr):
    conv_blocks = []
    for j in range(_N_CONV_BLOCKS):
        taps = [convw_ref[k:k + 1, j * LANES:(j + 1) * LANES] for k in range(3)]
        chunks = []
        for r0 in range(0, TILE, ROW_CHUNK):
            f = _shifted_rows(mix_scr, _N_POOL_BLOCKS + j, -1, r0, ROW_CHUNK + SUBLANES)
            chunks.append((f * taps[0] + _up(f, 1) * taps[1] + _up(f, 2) * taps[2])[0:ROW_CHUNK])
        conv_blocks.append(jnp.concatenate(chunks, axis=0))
    return jnp.concatenate(conv_blocks, axis=-1)


def _layer_kernel(xprev_ref, x_ref, xnext_ref, p_ref, gpre_ref, win_ref, wpool_ref,
                  pscale_ref, convw_ref, wout_ref, gpost_ref, wple_ref, wgate_ref,
                  gple_ref, out_ref, h_scr, mix_scr, *, seq_len):
    s = pl.program_id(1)
    bf16 = jnp.bfloat16
    _pre_norm_stage(s, xprev_ref, x_ref, xnext_ref, gpre_ref, h_scr,
                    tiles_per_seq=seq_len // TILE)

    _project_values(win_ref, h_scr, mix_scr)
    _project_conv_inputs(win_ref, h_scr, mix_scr)
    a_scale, b_scale = _project_gates(win_ref, pscale_ref, h_scr)
    ya = _pool_mixer(s, wpool_ref, mix_scr, seq_len=seq_len) * a_scale
    yb = _conv_mixer(convw_ref, mix_scr) * b_scale

    o = _dot(jnp.concatenate([ya, yb], axis=-1).astype(bf16), wout_ref[...])
    x1 = x_ref[0] + o * _rms_scale(o) * gpost_ref[...]

    gate = jax.nn.sigmoid(_dot(x1.astype(bf16), wgate_ref[...]))
    e = _dot(p_ref[0].astype(bf16), wple_ref[...]) * gate
    out_ref[0] = x1 + e * _rms_scale(e) * gple_ref[...]


def _resident(shape):
    return pl.BlockSpec(shape, lambda b, s: (0,) * len(shape),
                        pipeline_mode=pl.Buffered(1))


def _layer(x, p, g_pre, w_in, w_pool2, pool_scale, conv_w, w_out, g_post,
           w_ple, w_gate, g_ple):
    batch, seq_len, _ = x.shape
    assert seq_len % TILE == 0 and TILE % HALO == 0 and TILE % ROW_CHUNK == 0
    n_tiles = seq_len // TILE
    halo_per_tile = TILE // HALO
    n_halo_blocks = seq_len // HALO

    def prev_halo(b, s):
        return (b, jnp.maximum(s * halo_per_tile - 1, 0), 0)

    def next_halo(b, s):
        return (b, jnp.minimum((s + 1) * halo_per_tile, n_halo_blocks - 1), 0)

    def tile_block(b, s):
        return (b, s, 0)

    return pl.pallas_call(
        partial(_layer_kernel, seq_len=seq_len),
        grid=(batch, n_tiles),
        in_specs=[
            pl.BlockSpec((1, HALO, D_MODEL), prev_halo),
            pl.BlockSpec((1, TILE, D_MODEL), tile_block),
            pl.BlockSpec((1, HALO, D_MODEL), next_halo),
            pl.BlockSpec((1, TILE, D_PLE), tile_block),
            _resident(g_pre.shape),
            _resident(w_in.shape),
            _resident(w_pool2.shape),
            _resident(pool_scale.shape),
            _resident(conv_w.shape),
            _resident(w_out.shape),
            _resident(g_post.shape),
            _resident(w_ple.shape),
            _resident(w_gate.shape),
            _resident(g_ple.shape),
        ],
        out_specs=pl.BlockSpec((1, TILE, D_MODEL), tile_block),
        out_shape=jax.ShapeDtypeStruct(x.shape, x.dtype),
        scratch_shapes=[
            pltpu.VMEM((TILE + 2 * HALO, D_MODEL), jnp.bfloat16),
            pltpu.VMEM((_N_POOL_BLOCKS + _N_CONV_BLOCKS, TILE + 2 * HALO, LANES), jnp.float32),
        ],
        compiler_params=pltpu.CompilerParams(
            dimension_semantics=("arbitrary", "arbitrary"),
            vmem_limit_bytes=VMEM_LIMIT_BYTES),
        name="encoder_layer",
    )(x, x, x, p, g_pre, w_in, w_pool2, pool_scale, conv_w, w_out, g_post,
      w_ple, w_gate, g_ple)


def _interleave_pairs(w_in):
    def blocks(start):
        return [w_in[:, start + i * _PAIR:start + (i + 1) * _PAIR] for i in range(_N_PAIRS)]

    cols = [w_in[:, _A_VAL:_A_VAL + D_POOL], w_in[:, _A_GATE:_A_GATE + D_POOL]]
    for first, second in ((_B_C, _B_U), (_B_B, _B_GATE)):
        for a, b in zip(blocks(first), blocks(second)):
            cols += [a, b]
    return jnp.concatenate(cols, axis=1)


def _pair_block_diag(w_pool):
    z = jnp.zeros_like(w_pool[0])
    return jnp.stack([jnp.block([[w_pool[0], z], [z, w_pool[1]]]),
                      jnp.block([[w_pool[2], z], [z, w_pool[3]]])])


def kernel(x_prompt, x_sample, p_prompt, p_sample, g_pre, w_in, w_pool, pool_scale, conv_w, w_out, g_post, w_ple, w_ple_gate, g_ple):
    bf16 = jnp.bfloat16
    depth = g_pre.shape[0]
    xs = [x_prompt, x_sample]
    for i in range(depth):
        weights = (
            g_pre[i][None, :], _interleave_pairs(w_in[i]).astype(bf16),
            _pair_block_diag(w_pool[i]).astype(bf16), pool_scale[i][None, :],
            conv_w[i], w_out[i].astype(bf16), g_post[i][None, :],
            w_ple[i].astype(bf16), w_ple_gate[i].astype(bf16), g_ple[i][None, :])
        xs = [_layer(x, p[i], *weights) for x, p in zip(xs, (p_prompt, p_sample))]
    return tuple(xs)
```

```python
from functools import partial

import jax
import jax.numpy as jnp
from jax.experimental import pallas as pl
from jax.experimental.pallas import tpu as pltpu

D_MODEL = 1024
D_POOL = 512
D_CONV = 512
POOL_WINDOWS = (2, 4, 8, 16)
LANES = 128
SUBLANES = 8
D_PLE = 256
EPS = 1e-6

HALO = 16
TILE = 1024
ROW_CHUNK = 128
VMEM_LIMIT_BYTES = 58 * 1024 * 1024

_A_VAL, _A_GATE, _B_U, _B_B, _B_C, _B_GATE = (i * D_POOL for i in range(6))
_PAIR = 256
_N_PAIRS = D_CONV // _PAIR
_CU_PAIRS = 2 * D_POOL
_BG_PAIRS = _CU_PAIRS + 2 * D_CONV
_N_POOL_BLOCKS = D_POOL // LANES
_N_CONV_BLOCKS = D_CONV // LANES


def _rms_scale(v):
    return jax.lax.rsqrt(jnp.mean(v * v, axis=-1, keepdims=True) + EPS)


_dot = partial(jnp.dot, preferred_element_type=jnp.float32)


def _pre_norm_stage(s, xprev_ref, x_ref, xnext_ref, gpre_ref, h_scr, *, tiles_per_seq):
    g_pre = gpre_ref[...]

    def pre_norm(v):
        return (v * _rms_scale(v) * g_pre).astype(jnp.bfloat16)

    before = pre_norm(xprev_ref[0])
    after = pre_norm(xnext_ref[0])
    h_scr[0:HALO, :] = jnp.where(s == 0, jnp.zeros_like(before), before)
    h_scr[HALO:HALO + TILE, :] = pre_norm(x_ref[0])
    h_scr[HALO + TILE:, :] = jnp.where(s == tiles_per_seq - 1, jnp.zeros_like(after), after)


def _window_counts(row0, left, right, seq_len):
    pos = row0 + jax.lax.broadcasted_iota(jnp.int32, (SUBLANES, LANES), 0)
    lo = jnp.maximum(pos - left, 0)
    hi = jnp.minimum(pos + right + 1, seq_len)
    return (hi - lo).astype(jnp.float32)


def _project_values(win_ref, h_scr, mix_scr):
    a_val = _dot(h_scr[...], win_ref[:, _A_VAL:_A_VAL + D_POOL])
    for j in range(_N_POOL_BLOCKS):
        mix_scr[j] = a_val[:, j * LANES:(j + 1) * LANES]


def _project_conv_inputs(win_ref, h_scr, mix_scr):
    blocks_per_pair = _PAIR // LANES
    h_all = h_scr[...]
    for i in range(_N_PAIRS):
        pair = _dot(h_all, win_ref[:, _CU_PAIRS + 2 * i * _PAIR:_CU_PAIRS + 2 * (i + 1) * _PAIR])
        cu = pair[:, 0:_PAIR] * pair[:, _PAIR:]
        for j in range(blocks_per_pair):
            mix_scr[_N_POOL_BLOCKS + i * blocks_per_pair + j] = (
                cu[:, j * LANES:(j + 1) * LANES])


def _project_gates(win_ref, pscale_ref, h_scr):
    h_mid = h_scr[HALO:HALO + TILE, :]
    a_scale = pscale_ref[...] * jax.nn.silu(
        _dot(h_mid, win_ref[:, _A_GATE:_A_GATE + D_POOL]))
    b_scales = []
    for i in range(_N_PAIRS):
        pair = _dot(h_mid, win_ref[:, _BG_PAIRS + 2 * i * _PAIR:_BG_PAIRS + 2 * (i + 1) * _PAIR])
        b_scales.append(pair[:, 0:_PAIR] * jax.nn.silu(pair[:, _PAIR:]))
    return a_scale, jnp.concatenate(b_scales, axis=-1)


def _shifted_rows(mix_scr, block, first, chunk_row0, n_rows):
    return mix_scr[block, pl.ds(HALO + first + chunk_row0, n_rows), :]


def _up(v, k):
    return pltpu.roll(v, v.shape[0] - k, axis=0)


def _pool_mixer(s, wpool_ref, mix_scr, *, seq_len):
    seq_row0 = s * TILE
    top = slice(0, SUBLANES)
    mid = slice(SUBLANES, TILE - SUBLANES)
    bot = slice(TILE - SUBLANES, TILE)
    diffs = []
    for g, w in enumerate(POOL_WINDOWS):
        left, right = (w - 1) // 2, w // 2
        pad = SUBLANES * pl.cdiv(w - 1, SUBLANES)
        sums = []
        for r0 in range(0, TILE, ROW_CHUNK):
            f = _shifted_rows(mix_scr, g, -left, r0, ROW_CHUNK + pad)
            span = 1
            while span < w:
                f = f + _up(f, span)
                span *= 2
            sums.append(f[0:ROW_CHUNK])
        total = jnp.concatenate(sums, axis=0)
        centre = _shifted_rows(mix_scr, g, 0, 0, TILE)
        diffs.append(jnp.concatenate([
            total[top] / _window_counts(seq_row0, left, right, seq_len) - centre[top],
            total[mid] * (1.0 / w) - centre[mid],
            total[bot] / _window_counts(seq_row0 + TILE - SUBLANES, left, right, seq_len)
            - centre[bot],
        ], axis=0))
    d = jnp.concatenate(diffs, axis=-1).astype(jnp.bfloat16)
    half = D_POOL // 2
    return jnp.concatenate(
        [_dot(d[:, 0:half], wpool_ref[0]), _dot(d[:, half:], wpool_ref[1])], axis=-1)


def _conv_mixer(convw_ref, mix_scr):
    conv_blocks = []
    for j in range(_N_CONV_BLOCKS):
        taps = [convw_ref[k:k + 1, j * LANES:(j + 1) * LANES] for k in range(3)]
        chunks = []
        for r0 in range(0, TILE, ROW_CHUNK):
            f = _shifted_rows(mix_scr, _N_POOL_BLOCKS + j, -1, r0, ROW_CHUNK + SUBLANES)
            chunks.append((f * taps[0] + _up(f, 1) * taps[1] + _up(f, 2) * taps[2])[0:ROW_CHUNK])
        conv_blocks.append(jnp.concatenate(chunks, axis=0))
    return jnp.concatenate(conv_blocks, axis=-1)


def _layer_kernel(xprev_ref, x_ref, xnext_ref, p_ref, gpre_ref, win_ref, wpool_ref,
                  pscale_ref, convw_ref, wout_ref, gpost_ref, wple_ref, wgate_ref,
                  gple_ref, out_ref, h_scr, mix_scr, *, seq_len):
    s = pl.program_id(1)
    bf16 = jnp.bfloat16
    ple = _dot(p_ref[0].astype(bf16), wple_ref[...])
    _pre_norm_stage(s, xprev_ref, x_ref, xnext_ref, gpre_ref, h_scr,
                    tiles_per_seq=seq_len // TILE)

    _project_values(win_ref, h_scr, mix_scr)
    _project_conv_inputs(win_ref, h_scr, mix_scr)
    a_scale, b_scale = _project_gates(win_ref, pscale_ref, h_scr)
    ya = _pool_mixer(s, wpool_ref, mix_scr, seq_len=seq_len) * a_scale
    yb = _conv_mixer(convw_ref, mix_scr) * b_scale

    o = _dot(jnp.concatenate([ya, yb], axis=-1).astype(bf16), wout_ref[...])
    x1 = x_ref[0] + o * _rms_scale(o) * gpost_ref[...]

    gate = jax.nn.sigmoid(_dot(x1.astype(bf16), wgate_ref[...]))
    e = ple * gate
    out_ref[0] = x1 + e * _rms_scale(e) * gple_ref[...]


def _resident(shape):
    return pl.BlockSpec(shape, lambda b, s: (0,) * len(shape),
                        pipeline_mode=pl.Buffered(1))


def _layer(x, p, g_pre, w_in, w_pool2, pool_scale, conv_w, w_out, g_post,
           w_ple, w_gate, g_ple):
    batch, seq_len, _ = x.shape
    assert seq_len % TILE == 0 and TILE % HALO == 0 and TILE % ROW_CHUNK == 0
    n_tiles = seq_len // TILE
    halo_per_tile = TILE // HALO
    n_halo_blocks = seq_len // HALO

    def prev_halo(b, s):
        return (b, jnp.maximum(s * halo_per_tile - 1, 0), 0)

    def next_halo(b, s):
        return (b, jnp.minimum((s + 1) * halo_per_tile, n_halo_blocks - 1), 0)

    def tile_block(b, s):
        return (b, s, 0)

    return pl.pallas_call(
        partial(_layer_kernel, seq_len=seq_len),
        grid=(batch, n_tiles),
        in_specs=[
            pl.BlockSpec((1, HALO, D_MODEL), prev_halo),
            pl.BlockSpec((1, TILE, D_MODEL), tile_block),
            pl.BlockSpec((1, HALO, D_MODEL), next_halo),
            pl.BlockSpec((1, TILE, D_PLE), tile_block),
            _resident(g_pre.shape),
            _resident(w_in.shape),
            _resident(w_pool2.shape),
            _resident(pool_scale.shape),
            _resident(conv_w.shape),
            _resident(w_out.shape),
            _resident(g_post.shape),
            _resident(w_ple.shape),
            _resident(w_gate.shape),
            _resident(g_ple.shape),
        ],
        out_specs=pl.BlockSpec((1, TILE, D_MODEL), tile_block),
        out_shape=jax.ShapeDtypeStruct(x.shape, x.dtype),
        scratch_shapes=[
            pltpu.VMEM((TILE + 2 * HALO, D_MODEL), jnp.bfloat16),
            pltpu.VMEM((_N_POOL_BLOCKS + _N_CONV_BLOCKS, TILE + 2 * HALO, LANES), jnp.float32),
        ],
        compiler_params=pltpu.CompilerParams(
            dimension_semantics=("arbitrary", "arbitrary"),
            vmem_limit_bytes=VMEM_LIMIT_BYTES),
        name="encoder_layer",
    )(x, x, x, p, g_pre, w_in, w_pool2, pool_scale, conv_w, w_out, g_post,
      w_ple, w_gate, g_ple)


def _interleave_pairs(w_in):
    def blocks(start):
        return [w_in[:, start + i * _PAIR:start + (i + 1) * _PAIR] for i in range(_N_PAIRS)]

    cols = [w_in[:, _A_VAL:_A_VAL + D_POOL], w_in[:, _A_GATE:_A_GATE + D_POOL]]
    for first, second in ((_B_C, _B_U), (_B_B, _B_GATE)):
        for a, b in zip(blocks(first), blocks(second)):
            cols += [a, b]
    return jnp.concatenate(cols, axis=1)


def _pair_block_diag(w_pool):
    z = jnp.zeros_like(w_pool[0])
    return jnp.stack([jnp.block([[w_pool[0], z], [z, w_pool[1]]]),
                      jnp.block([[w_pool[2], z], [z, w_pool[3]]])])


def kernel(x_prompt, x_sample, p_prompt, p_sample, g_pre, w_in, w_pool, pool_scale, conv_w, w_out, g_post, w_ple, w_ple_gate, g_ple):
    bf16 = jnp.bfloat16
    depth = g_pre.shape[0]
    xs = [x_prompt, x_sample]
    for i in range(depth):
        weights = (
            g_pre[i][None, :], _interleave_pairs(w_in[i]).astype(bf16),
            _pair_block_diag(w_pool[i]).astype(bf16), pool_scale[i][None, :],
            conv_w[i], w_out[i].astype(bf16), g_post[i][None, :],
            w_ple[i].astype(bf16), w_ple_gate[i].astype(bf16), g_ple[i][None, :])
        xs = [_layer(x, p[i], *weights) for x, p in zip(xs, (p_prompt, p_sample))]
    return tuple(xs)
```

```python
from functools import partial

import jax
import jax.numpy as jnp
from jax.experimental import pallas as pl
from jax.experimental.pallas import tpu as pltpu

D_MODEL = 1024
D_POOL = 512
D_CONV = 512
POOL_WINDOWS = (2, 4, 8, 16)
LANES = 128
SUBLANES = 8
D_PLE = 256
EPS = 1e-6

HALO = 16
TILE = 1024
ROW_CHUNK = 128
VMEM_LIMIT_BYTES = 58 * 1024 * 1024

_A_VAL, _A_GATE, _B_U, _B_B, _B_C, _B_GATE = (i * D_POOL for i in range(6))
_PAIR = 256
_N_PAIRS = D_CONV // _PAIR
_CU_PAIRS = 0
_BG_PAIRS = _CU_PAIRS + 2 * D_CONV
_N_POOL_BLOCKS = D_POOL // LANES
_N_CONV_BLOCKS = D_CONV // LANES


def _rms_scale(v):
    return jax.lax.rsqrt(jnp.mean(v * v, axis=-1, keepdims=True) + EPS)


_dot = partial(jnp.dot, preferred_element_type=jnp.float32)


def _pre_norm_stage(s, xprev_ref, x_ref, xnext_ref, gpre_ref, h_scr, *, tiles_per_seq):
    g_pre = gpre_ref[...]

    def pre_norm(v):
        return (v * _rms_scale(v) * g_pre).astype(jnp.bfloat16)

    before = pre_norm(xprev_ref[0])
    after = pre_norm(xnext_ref[0])
    h_scr[0:HALO, :] = jnp.where(s == 0, jnp.zeros_like(before), before)
    h_scr[HALO:HALO + TILE, :] = pre_norm(x_ref[0])
    h_scr[HALO + TILE:, :] = jnp.where(s == tiles_per_seq - 1, jnp.zeros_like(after), after)


def _window_counts(row0, left, right, seq_len):
    pos = row0 + jax.lax.broadcasted_iota(jnp.int32, (SUBLANES, LANES), 0)
    lo = jnp.maximum(pos - left, 0)
    hi = jnp.minimum(pos + right + 1, seq_len)
    return (hi - lo).astype(jnp.float32)


def _project_values(win_ref, h_scr, mix_scr):
    a_val = _dot(h_scr[...], win_ref[:, _A_VAL:_A_VAL + D_POOL])
    for j in range(_N_POOL_BLOCKS):
        mix_scr[j] = a_val[:, j * LANES:(j + 1) * LANES]


def _pair_conv_columns(win_ref, wpair_scr):
    for base, (first, second) in ((_CU_PAIRS, (_B_C, _B_U)), (_BG_PAIRS, (_B_B, _B_GATE))):
        for i in range(_N_PAIRS):
            for k, src in enumerate((first, second)):
                dst = base + (2 * i + k) * _PAIR
                wpair_scr[:, dst:dst + _PAIR] = win_ref[:, src + i * _PAIR:src + (i + 1) * _PAIR]


def _project_conv_inputs(wpair_scr, h_scr, mix_scr):
    blocks_per_pair = _PAIR // LANES
    h_all = h_scr[...]
    for i in range(_N_PAIRS):
        pair = _dot(h_all, wpair_scr[:, _CU_PAIRS + 2 * i * _PAIR:_CU_PAIRS + 2 * (i + 1) * _PAIR])
        cu = pair[:, 0:_PAIR] * pair[:, _PAIR:]
        for j in range(blocks_per_pair):
            mix_scr[_N_POOL_BLOCKS + i * blocks_per_pair + j] = (
                cu[:, j * LANES:(j + 1) * LANES])


def _project_gates(win_ref, wpair_scr, pscale_ref, h_scr):
    h_mid = h_scr[HALO:HALO + TILE, :]
    a_scale = pscale_ref[...] * jax.nn.silu(
        _dot(h_mid, win_ref[:, _A_GATE:_A_GATE + D_POOL]))
    b_scales = []
    for i in range(_N_PAIRS):
        pair = _dot(h_mid, wpair_scr[:, _BG_PAIRS + 2 * i * _PAIR:_BG_PAIRS + 2 * (i + 1) * _PAIR])
        b_scales.append(pair[:, 0:_PAIR] * jax.nn.silu(pair[:, _PAIR:]))
    return a_scale, jnp.concatenate(b_scales, axis=-1)


def _shifted_rows(mix_scr, block, first, chunk_row0, n_rows):
    return mix_scr[block, pl.ds(HALO + first + chunk_row0, n_rows), :]


def _up(v, k):
    return pltpu.roll(v, v.shape[0] - k, axis=0)


def _pool_mixer(s, wpool_ref, mix_scr, *, seq_len):
    seq_row0 = s * TILE
    top = slice(0, SUBLANES)
    mid = slice(SUBLANES, TILE - SUBLANES)
    bot = slice(TILE - SUBLANES, TILE)
    diffs = []
    for g, w in enumerate(POOL_WINDOWS):
        left, right = (w - 1) // 2, w // 2
        pad = SUBLANES * pl.cdiv(w - 1, SUBLANES)
        sums = []
        for r0 in range(0, TILE, ROW_CHUNK):
            f = _shifted_rows(mix_scr, g, -left, r0, ROW_CHUNK + pad)
            span = 1
            while span < w:
                f = f + _up(f, span)
                span *= 2
            sums.append(f[0:ROW_CHUNK])
        total = jnp.concatenate(sums, axis=0)
        centre = _shifted_rows(mix_scr, g, 0, 0, TILE)
        diffs.append(jnp.concatenate([
            total[top] / _window_counts(seq_row0, left, right, seq_len) - centre[top],
            total[mid] * (1.0 / w) - centre[mid],
            total[bot] / _window_counts(seq_row0 + TILE - SUBLANES, left, right, seq_len)
            - centre[bot],
        ], axis=0))
    d = jnp.concatenate(diffs, axis=-1).astype(jnp.bfloat16)
    half = D_POOL // 2
    return jnp.concatenate(
        [_dot(d[:, 0:half], wpool_ref[0]), _dot(d[:, half:], wpool_ref[1])], axis=-1)


def _conv_mixer(convw_ref, mix_scr):
    conv_blocks = []
    for j in range(_N_CONV_BLOCKS):
        taps = [convw_ref[k:k + 1, j * LANES:(j + 1) * LANES] for k in range(3)]
        chunks = []
        for r0 in range(0, TILE, ROW_CHUNK):
            f = _shifted_rows(mix_scr, _N_POOL_BLOCKS + j, -1, r0, ROW_CHUNK + SUBLANES)
            chunks.append((f * taps[0] + _up(f, 1) * taps[1] + _up(f, 2) * taps[2])[0:ROW_CHUNK])
        conv_blocks.append(jnp.concatenate(chunks, axis=0))
    return jnp.concatenate(conv_blocks, axis=-1)


def _layer_kernel(xprev_ref, x_ref, xnext_ref, p_ref, gpre_ref, win_ref, wpool_ref,
                  pscale_ref, convw_ref, wout_ref, gpost_ref, wple_ref, wgate_ref,
                  gple_ref, out_ref, h_scr, mix_scr, wpair_scr, *, seq_len):
    s = pl.program_id(1)
    bf16 = jnp.bfloat16

    @pl.when((pl.program_id(0) == 0) & (s == 0))
    def _():
        _pair_conv_columns(win_ref, wpair_scr)

    ple = _dot(p_ref[0].astype(bf16), wple_ref[...])
    _pre_norm_stage(s, xprev_ref, x_ref, xnext_ref, gpre_ref, h_scr,
                    tiles_per_seq=seq_len // TILE)

    _project_values(win_ref, h_scr, mix_scr)
    _project_conv_inputs(wpair_scr, h_scr, mix_scr)
    a_scale, b_scale = _project_gates(win_ref, wpair_scr, pscale_ref, h_scr)
    ya = _pool_mixer(s, wpool_ref, mix_scr, seq_len=seq_len) * a_scale
    yb = _conv_mixer(convw_ref, mix_scr) * b_scale

    o = _dot(jnp.concatenate([ya, yb], axis=-1).astype(bf16), wout_ref[...])
    x1 = x_ref[0] + o * _rms_scale(o) * gpost_ref[...]

    gate = jax.nn.sigmoid(_dot(x1.astype(bf16), wgate_ref[...]))
    e = ple * gate
    out_ref[0] = x1 + e * _rms_scale(e) * gple_ref[...]


def _resident(shape):
    return pl.BlockSpec(shape, lambda b, s: (0,) * len(shape),
                        pipeline_mode=pl.Buffered(1))


def _layer(x, p, g_pre, w_in, w_pool2, pool_scale, conv_w, w_out, g_post,
           w_ple, w_gate, g_ple):
    batch, seq_len, _ = x.shape
    assert seq_len % TILE == 0 and TILE % HALO == 0 and TILE % ROW_CHUNK == 0
    n_tiles = seq_len // TILE
    halo_per_tile = TILE // HALO
    n_halo_blocks = seq_len // HALO

    def prev_halo(b, s):
        return (b, jnp.maximum(s * halo_per_tile - 1, 0), 0)

    def next_halo(b, s):
        return (b, jnp.minimum((s + 1) * halo_per_tile, n_halo_blocks - 1), 0)

    def tile_block(b, s):
        return (b, s, 0)

    return pl.pallas_call(
        partial(_layer_kernel, seq_len=seq_len),
        grid=(batch, n_tiles),
        in_specs=[
            pl.BlockSpec((1, HALO, D_MODEL), prev_halo),
            pl.BlockSpec((1, TILE, D_MODEL), tile_block),
            pl.BlockSpec((1, HALO, D_MODEL), next_halo),
            pl.BlockSpec((1, TILE, D_PLE), tile_block),
            _resident(g_pre.shape),
            _resident(w_in.shape),
            _resident(w_pool2.shape),
            _resident(pool_scale.shape),
            _resident(conv_w.shape),
            _resident(w_out.shape),
            _resident(g_post.shape),
            _resident(w_ple.shape),
            _resident(w_gate.shape),
            _resident(g_ple.shape),
        ],
        out_specs=pl.BlockSpec((1, TILE, D_MODEL), tile_block),
        out_shape=jax.ShapeDtypeStruct(x.shape, x.dtype),
        scratch_shapes=[
            pltpu.VMEM((TILE + 2 * HALO, D_MODEL), jnp.bfloat16),
            pltpu.VMEM((_N_POOL_BLOCKS + _N_CONV_BLOCKS, TILE + 2 * HALO, LANES), jnp.float32),
            pltpu.VMEM((D_MODEL, 4 * D_CONV), jnp.bfloat16),
        ],
        compiler_params=pltpu.CompilerParams(
            dimension_semantics=("arbitrary", "arbitrary"),
            vmem_limit_bytes=VMEM_LIMIT_BYTES),
        name="encoder_layer",
    )(x, x, x, p, g_pre, w_in, w_pool2, pool_scale, conv_w, w_out, g_post,
      w_ple, w_gate, g_ple)


def _pair_block_diag(w_pool):
    z = jnp.zeros_like(w_pool[0])
    return jnp.stack([jnp.block([[w_pool[0], z], [z, w_pool[1]]]),
                      jnp.block([[w_pool[2], z], [z, w_pool[3]]])])


def kernel(x_prompt, x_sample, p_prompt, p_sample, g_pre, w_in, w_pool, pool_scale, conv_w, w_out, g_post, w_ple, w_ple_gate, g_ple):
    bf16 = jnp.bfloat16
    depth = g_pre.shape[0]
    xs = [x_prompt, x_sample]
    for i in range(depth):
        weights = (
            g_pre[i][None, :], w_in[i].astype(bf16),
            _pair_block_diag(w_pool[i]).astype(bf16), pool_scale[i][None, :],
            conv_w[i], w_out[i].astype(bf16), g_post[i][None, :],
            w_ple[i].astype(bf16), w_ple_gate[i].astype(bf16), g_ple[i][None, :])
        xs = [_layer(x, p[i], *weights) for x, p in zip(xs, (p_prompt, p_sample))]
    return tuple(xs)
```

```python
from functools import partial

import jax
import jax.numpy as jnp
from jax.experimental import pallas as pl
from jax.experimental.pallas import tpu as pltpu

D_MODEL = 1024
D_POOL = 512
D_CONV = 512
POOL_WINDOWS = (2, 4, 8, 16)
LANES = 128
SUBLANES = 8
D_PLE = 256
EPS = 1e-6

HALO = 16
TILE = 1024
ROW_CHUNK = 128
VMEM_LIMIT_BYTES = 58 * 1024 * 1024

_A_VAL, _A_GATE, _B_U, _B_B, _B_C, _B_GATE = (i * D_POOL for i in range(6))
_PAIR = 256
_N_PAIRS = D_CONV // _PAIR
_CU_PAIRS = 0
_BG_PAIRS = _CU_PAIRS + 2 * D_CONV
_N_PROJECTION_DOTS = 2 + 2 * _N_PAIRS
_FINISH_EDGES = tuple(SUBLANES * round(i * TILE / _N_PROJECTION_DOTS / SUBLANES)
                      for i in range(_N_PROJECTION_DOTS + 1))
_N_POOL_BLOCKS = D_POOL // LANES
_N_CONV_BLOCKS = D_CONV // LANES


def _rms_scale(v):
    return jax.lax.rsqrt(jnp.mean(v * v, axis=-1, keepdims=True) + EPS)


_dot = partial(jnp.dot, preferred_element_type=jnp.float32)


def _pre_norm_stage(seq_tile, xprev_ref, x_ref, xnext_ref, gpre_ref, h_scr, *, tiles_per_seq):
    g_pre = gpre_ref[...]

    def pre_norm(v):
        return (v * _rms_scale(v) * g_pre).astype(jnp.bfloat16)

    before = pre_norm(xprev_ref[...])
    after = pre_norm(xnext_ref[...])
    h_scr[0:HALO, :] = jnp.where(seq_tile == 0, jnp.zeros_like(before), before)
    h_scr[HALO:HALO + TILE, :] = pre_norm(x_ref[...])
    h_scr[HALO + TILE:, :] = jnp.where(seq_tile == tiles_per_seq - 1, jnp.zeros_like(after), after)


def _window_counts(row0, left, right, seq_len):
    pos = row0 + jax.lax.broadcasted_iota(jnp.int32, (SUBLANES, LANES), 0)
    lo = jnp.maximum(pos - left, 0)
    hi = jnp.minimum(pos + right + 1, seq_len)
    return (hi - lo).astype(jnp.float32)


def _pair_conv_columns(win_ref, wpair_scr):
    for base, (first, second) in ((_CU_PAIRS, (_B_C, _B_U)), (_BG_PAIRS, (_B_B, _B_GATE))):
        for i in range(_N_PAIRS):
            for k, src in enumerate((first, second)):
                dst = base + (2 * i + k) * _PAIR
                wpair_scr[:, dst:dst + _PAIR] = win_ref[:, src + i * _PAIR:src + (i + 1) * _PAIR]


def _project_values(win_ref, h_scr, mix_scr, tie):
    a_val = tie(_dot(h_scr[...], win_ref[:, _A_VAL:_A_VAL + D_POOL]))
    for j in range(_N_POOL_BLOCKS):
        mix_scr[j] = a_val[:, j * LANES:(j + 1) * LANES]


def _project_conv_inputs(wpair_scr, h_scr, mix_scr, ties):
    blocks_per_pair = _PAIR // LANES
    h_all = h_scr[...]
    for i in range(_N_PAIRS):
        pair = ties[i](_dot(
            h_all, wpair_scr[:, _CU_PAIRS + 2 * i * _PAIR:_CU_PAIRS + 2 * (i + 1) * _PAIR]))
        cu = pair[:, 0:_PAIR] * pair[:, _PAIR:]
        for j in range(blocks_per_pair):
            mix_scr[_N_POOL_BLOCKS + i * blocks_per_pair + j] = (
                cu[:, j * LANES:(j + 1) * LANES])


def _project_gates(win_ref, wpair_scr, pscale_ref, h_scr, ties):
    h_mid = h_scr[HALO:HALO + TILE, :]
    a_scale = pscale_ref[...] * jax.nn.silu(
        ties[0](_dot(h_mid, win_ref[:, _A_GATE:_A_GATE + D_POOL])))
    b_scales = []
    for i in range(_N_PAIRS):
        pair = ties[1 + i](_dot(
            h_mid, wpair_scr[:, _BG_PAIRS + 2 * i * _PAIR:_BG_PAIRS + 2 * (i + 1) * _PAIR]))
        b_scales.append(pair[:, 0:_PAIR] * jax.nn.silu(pair[:, _PAIR:]))
    return a_scale, jnp.concatenate(b_scales, axis=-1)


def _shifted_rows(mix_scr, block, first, chunk_row0, n_rows):
    return mix_scr[block, pl.ds(HALO + first + chunk_row0, n_rows), :]


def _up(v, k):
    return pltpu.roll(v, v.shape[0] - k, axis=0)


def _pool_mixer(seq_tile, wpool_ref, mix_scr, *, seq_len):
    seq_row0 = seq_tile * TILE
    top = slice(0, SUBLANES)
    mid = slice(SUBLANES, TILE - SUBLANES)
    bot = slice(TILE - SUBLANES, TILE)
    diffs = []
    for g, w in enumerate(POOL_WINDOWS):
        left, right = (w - 1) // 2, w // 2
        pad = SUBLANES * pl.cdiv(w - 1, SUBLANES)
        sums = []
        for r0 in range(0, TILE, ROW_CHUNK):
            f = _shifted_rows(mix_scr, g, -left, r0, ROW_CHUNK + pad)
            span = 1
            while span < w:
                f = f + _up(f, span)
                span *= 2
            sums.append(f[0:ROW_CHUNK])
        total = jnp.concatenate(sums, axis=0)
        centre = _shifted_rows(mix_scr, g, 0, 0, TILE)
        diffs.append(jnp.concatenate([
            total[top] / _window_counts(seq_row0, left, right, seq_len) - centre[top],
            total[mid] * (1.0 / w) - centre[mid],
            total[bot] / _window_counts(seq_row0 + TILE - SUBLANES, left, right, seq_len)
            - centre[bot],
        ], axis=0))
    d = jnp.concatenate(diffs, axis=-1).astype(jnp.bfloat16)
    half = D_POOL // 2
    return jnp.concatenate(
        [_dot(d[:, 0:half], wpool_ref[0]), _dot(d[:, half:], wpool_ref[1])], axis=-1)


def _conv_mixer(convw_ref, mix_scr):
    conv_blocks = []
    for j in range(_N_CONV_BLOCKS):
        taps = [convw_ref[k:k + 1, j * LANES:(j + 1) * LANES] for k in range(3)]
        chunks = []
        for r0 in range(0, TILE, ROW_CHUNK):
            f = _shifted_rows(mix_scr, _N_POOL_BLOCKS + j, -1, r0, ROW_CHUNK + SUBLANES)
            chunks.append((f * taps[0] + _up(f, 1) * taps[1] + _up(f, 2) * taps[2])[0:ROW_CHUNK])
        conv_blocks.append(jnp.concatenate(chunks, axis=0))
    return jnp.concatenate(conv_blocks, axis=-1)


def _finish_previous_rows(rows, out_ref, gple_ref, x1_scr, e_scr):
    e = e_scr[rows, :]
    out_ref[rows, :] = x1_scr[rows, :] + e * _rms_scale(e) * gple_ref[...]


def _ordering_token(ref, zero_ref):
    zero = zero_ref[0]
    row = ref[pl.ds(zero, 1), :]
    return (jax.lax.bitcast_convert_type(row, jnp.int32) & zero).astype(jnp.float32)


def _tied_to(token, v):
    return jnp.concatenate([v[0:SUBLANES] + token[:, 0:v.shape[1]], v[SUBLANES:]], axis=0)


def _layer_kernel(zero_ref, xprev_ref, x_ref, xnext_ref, p_ref, gpre_ref, win_ref,
                  wpool_ref, pscale_ref, convw_ref, wout_ref, gpost_ref, wple_ref,
                  wgate_ref, gple_ref, out_ref, h_scr, mix_scr, wpair_scr, x1_scr, e_scr,
                  *, seq_len, n_tiles):
    t = pl.program_id(0)
    tiles_per_seq = seq_len // TILE
    bf16 = jnp.bfloat16

    @pl.when(t == 0)
    def _():
        x1_scr[...] = jnp.zeros_like(x1_scr)
        e_scr[...] = jnp.zeros_like(e_scr)
        _pair_conv_columns(win_ref, wpair_scr)

    @pl.when(t < n_tiles)
    def _():
        seq_tile = t % tiles_per_seq
        ple = _dot(p_ref[...].astype(bf16), wple_ref[...])
        _pre_norm_stage(seq_tile, xprev_ref, x_ref, xnext_ref, gpre_ref, h_scr,
                        tiles_per_seq=tiles_per_seq)

        ties = []
        for first, last in zip(_FINISH_EDGES, _FINISH_EDGES[1:]):
            _finish_previous_rows(slice(first, last), out_ref, gple_ref, x1_scr, e_scr)
            ties.append(partial(_tied_to, _ordering_token(out_ref, zero_ref)))

        _project_values(win_ref, h_scr, mix_scr, ties[0])
        _project_conv_inputs(wpair_scr, h_scr, mix_scr, ties[1:1 + _N_PAIRS])
        a_scale, b_scale = _project_gates(win_ref, wpair_scr, pscale_ref, h_scr, ties[1 + _N_PAIRS:])
        ya = _pool_mixer(seq_tile, wpool_ref, mix_scr, seq_len=seq_len) * a_scale
        yb = _conv_mixer(convw_ref, mix_scr) * b_scale

        o = _dot(jnp.concatenate([ya, yb], axis=-1).astype(bf16), wout_ref[...])
        x1 = x_ref[...] + o * _rms_scale(o) * gpost_ref[...]

        x1_scr[...] = x1
        gate = jax.nn.sigmoid(_dot(x1.astype(bf16), wgate_ref[...]))
        e_scr[...] = ple * gate

    @pl.when(t == n_tiles)
    def _():
        _finish_previous_rows(slice(None), out_ref, gple_ref, x1_scr, e_scr)


def _pipelined_kernel(zero_ref, x_hbm, p_hbm, gpre_ref, win_ref, wpool_ref, pscale_ref,
                      convw_ref, wout_ref, gpost_ref, wple_ref, wgate_ref, gple_ref, out_hbm,
                      *scratch, seq_len, n_tiles, tile_specs, out_spec):
    def step(xprev_ref, x_ref, xnext_ref, p_ref, out_ref):
        _layer_kernel(zero_ref, xprev_ref, x_ref, xnext_ref, p_ref, gpre_ref, win_ref,
                      wpool_ref, pscale_ref, convw_ref, wout_ref, gpost_ref, wple_ref,
                      wgate_ref, gple_ref, out_ref, *scratch, seq_len=seq_len, n_tiles=n_tiles)

    pltpu.emit_pipeline(step, grid=(n_tiles + 1,), in_specs=tile_specs, out_specs=[out_spec])(
        x_hbm, x_hbm, x_hbm, p_hbm, out_hbm)


def _layer(x, p, g_pre, w_in, w_pool2, pool_scale, conv_w, w_out, g_post,
           w_ple, w_gate, g_ple):
    batch, seq_len, _ = x.shape
    assert seq_len % TILE == 0 and TILE % HALO == 0 and TILE % ROW_CHUNK == 0
    n_rows = batch * seq_len
    n_tiles = n_rows // TILE
    halo_per_tile = TILE // HALO
    n_halo_blocks = n_rows // HALO
    x2 = x.reshape(n_rows, D_MODEL)
    p2 = p.reshape(n_rows, D_PLE)
    run_time_zero = jnp.zeros((1,), jnp.int32)

    def this_tile(t):
        return (jnp.minimum(t, n_tiles - 1), 0)

    def previous_tile(t):
        return (jnp.maximum(t - 1, 0), 0)

    def prev_halo(t):
        return (jnp.clip(t * halo_per_tile - 1, 0, n_halo_blocks - 1), 0)

    def next_halo(t):
        return (jnp.minimum((t + 1) * halo_per_tile, n_halo_blocks - 1), 0)

    tile_specs = [
        pl.BlockSpec((HALO, D_MODEL), prev_halo),
        pl.BlockSpec((TILE, D_MODEL), this_tile),
        pl.BlockSpec((HALO, D_MODEL), next_halo),
        pl.BlockSpec((TILE, D_PLE), this_tile),
    ]
    out_spec = pl.BlockSpec((TILE, D_MODEL), previous_tile)
    in_hbm = pl.BlockSpec(memory_space=pl.ANY)
    in_vmem = pl.BlockSpec(memory_space=pltpu.VMEM)
    out = pl.pallas_call(
        partial(_pipelined_kernel, seq_len=seq_len, n_tiles=n_tiles,
                tile_specs=tile_specs, out_spec=out_spec),
        in_specs=[pl.BlockSpec(memory_space=pltpu.SMEM), in_hbm, in_hbm] + [in_vmem] * 10,
        out_specs=pl.BlockSpec(memory_space=pl.ANY),
        out_shape=jax.ShapeDtypeStruct((n_rows, D_MODEL), x.dtype),
        scratch_shapes=[
            pltpu.VMEM((TILE + 2 * HALO, D_MODEL), jnp.bfloat16),
            pltpu.VMEM((_N_POOL_BLOCKS + _N_CONV_BLOCKS, TILE + 2 * HALO, LANES), jnp.float32),
            pltpu.VMEM((D_MODEL, 4 * D_CONV), jnp.bfloat16),
            pltpu.VMEM((TILE, D_MODEL), jnp.float32),
            pltpu.VMEM((TILE, D_MODEL), jnp.float32),
        ],
        compiler_params=pltpu.CompilerParams(vmem_limit_bytes=VMEM_LIMIT_BYTES),
        name="encoder_layer",
    )(run_time_zero, x2, p2, g_pre, w_in, w_pool2, pool_scale, conv_w, w_out,
      g_post, w_ple, w_gate, g_ple)
    return out.reshape(x.shape)


def _pair_block_diag(w_pool):
    z = jnp.zeros_like(w_pool[0])
    return jnp.stack([jnp.block([[w_pool[0], z], [z, w_pool[1]]]),
                      jnp.block([[w_pool[2], z], [z, w_pool[3]]])])


def kernel(x_prompt, x_sample, p_prompt, p_sample, g_pre, w_in, w_pool, pool_scale, conv_w, w_out, g_post, w_ple, w_ple_gate, g_ple):
    bf16 = jnp.bfloat16
    depth = g_pre.shape[0]
    xs = [x_prompt, x_sample]
    for i in range(depth):
        weights = (
            g_pre[i][None, :], w_in[i].astype(bf16),
            _pair_block_diag(w_pool[i]).astype(bf16), pool_scale[i][None, :],
            conv_w[i], w_out[i].astype(bf16), g_post[i][None, :],
            w_ple[i].astype(bf16), w_ple_gate[i].astype(bf16), g_ple[i][None, :])
        xs = [_layer(x, p[i], *weights) for x, p in zip(xs, (p_prompt, p_sample))]
    return tuple(xs)
```
